```python
import math
import jax, jax.numpy as jnp
from jax import lax
import numpy as np

D_MODEL = 1024
BATCH = 4
SEQ = 8192
DEPTH = 2

HEAD_DIM = 64
SWA_HEADS = 4
SWA_KV_HEADS = 2
SWA_WINDOW = 128
SWA_BLOCK = 128
MOBA_HEADS = 4
MOBA_BLOCK = 256
MOBA_TOPK = 3
MOBA_QCHUNK = 64
DN_HEADS = 4
DN_HEAD_DIM = 128
DN_CONV = 4
DN_CHUNK = 64
D_FF = 2816
FFN_CONV = 3

NORM_EPS = 1e-6
NEG_INF = -1e30

SWA_Q = SWA_HEADS * HEAD_DIM
SWA_KV = SWA_KV_HEADS * HEAD_DIM
MOBA_W = MOBA_HEADS * HEAD_DIM
DN_W = DN_HEADS * DN_HEAD_DIM
MIX_WIDTH = SWA_Q + MOBA_W + DN_W
IN_SPLIT_SIZES = (SWA_Q, SWA_KV, SWA_KV, MOBA_W, MOBA_W, MOBA_W, 3 * DN_W, DN_W, DN_HEADS, DN_HEADS)
IN_WIDTH = SWA_Q + 2 * SWA_KV + 3 * MOBA_W + 4 * DN_W + 2 * DN_HEADS

kernel_name = 'hybrid_swa_moba_gdn_block'


def rms_norm(x, w):
    xf = x.astype(jnp.float32)
    y = xf * lax.rsqrt(jnp.mean(xf * xf, axis=-1, keepdims=True) + NORM_EPS)
    return (y * w.astype(jnp.float32)).astype(x.dtype)


def l2_norm(x):
    return x * lax.rsqrt(jnp.sum(x * x, axis=-1, keepdims=True) + NORM_EPS)


def alibi_slopes():
    n = SWA_HEADS + MOBA_HEADS
    s = 2.0 ** (-8.0 * jnp.arange(1, n + 1, dtype=jnp.float32) / n)
    return s[0::2], s[1::2]


def causal_depthwise_conv(x, w):
    K = w.shape[0]
    S = x.shape[1]
    xp = jnp.pad(x, ((0, 0), (K - 1, 0), (0, 0)))
    y = xp[:, 0:S] * w[0]
    for i in range(1, K):
        y = y + xp[:, i:i + S] * w[i]
    return y


def swa_sink_attention(q, k, v, sinks, slopes):
    B, S, Hq, d = q.shape
    Hkv = k.shape[2]
    G = Hq // Hkv
    nb = S // SWA_BLOCK
    qb = q.reshape(B, nb, SWA_BLOCK, Hkv, G, d)
    kb = k.reshape(B, nb, SWA_BLOCK, Hkv, d)
    vb = v.reshape(B, nb, SWA_BLOCK, Hkv, d)

    def with_prev(t):
        prev = jnp.pad(t, ((0, 0), (1, 0), (0, 0), (0, 0), (0, 0)))[:, :-1]
        return jnp.concatenate([prev, t], axis=2)

    kk, vv = with_prev(kb), with_prev(vb)
    scores = jnp.einsum('bnqhgd,bnshd->bnhgqs', qb, kk).astype(jnp.float32) * (d ** -0.5)
    rel = jnp.arange(SWA_BLOCK)[:, None] + SWA_BLOCK - jnp.arange(2 * SWA_BLOCK)[None, :]
    kpos = (jnp.arange(nb) * SWA_BLOCK - SWA_BLOCK)[:, None] + jnp.arange(2 * SWA_BLOCK)[None, :]
    valid = ((rel >= 0) & (rel < SWA_WINDOW))[None] & (kpos >= 0)[:, None, :]
    m = slopes.astype(jnp.float32).reshape(Hkv, G)[:, :, None, None]
    scores = scores - m * rel.astype(jnp.float32)
    scores = jnp.where(valid[None, :, None, None], scores, NEG_INF)
    sink = sinks.astype(jnp.float32).reshape(Hkv, G)[:, :, None, None]
    mx = jnp.maximum(scores.max(axis=-1, keepdims=True), sink)
    p = jnp.exp(scores - mx)
    denom = p.sum(axis=-1, keepdims=True) + jnp.exp(sink - mx)
    probs = (p / denom).astype(v.dtype)
    out = jnp.einsum('bnhgqs,bnshd->bnqhgd', probs, vv)
    return out.reshape(B, S, Hq, d)


def moba_attention(q, k, v, slopes):
    B, S, H, d = q.shape
    S_pad = -(-S // MOBA_BLOCK) * MOBA_BLOCK
    pad = ((0, 0), (0, S_pad - S), (0, 0), (0, 0))
    q, k, v = jnp.pad(q, pad), jnp.pad(k, pad), jnp.pad(v, pad)
    nblk = S_pad // MOBA_BLOCK
    topk = min(MOBA_TOPK, nblk)
    kb = k.reshape(B, nblk, MOBA_BLOCK, H, d).transpose(0, 3, 1, 2, 4)
    vb = v.reshape(B, nblk, MOBA_BLOCK, H, d).transpose(0, 3, 1, 2, 4)
    k_mean = kb.astype(jnp.float32).mean(axis=3)
    nq = S_pad // MOBA_QCHUNK
    q_chunks = q.reshape(B, nq, MOBA_QCHUNK, H, d).transpose(1, 0, 3, 2, 4)
    bi = jnp.arange(B)[:, None, None, None]
    hi = jnp.arange(H)[None, :, None, None]
    m = slopes.astype(jnp.float32)[None, :, None, None]
    scale = d ** -0.5
    offs = jnp.arange(MOBA_BLOCK)

    def chunk_attend(args):
        qc, c = args
        t = c * MOBA_QCHUNK + jnp.arange(MOBA_QCHUNK)
        own = (c * MOBA_QCHUNK) // MOBA_BLOCK
        gate = jnp.einsum('bhqd,bhnd->bhqn', qc.astype(jnp.float32), k_mean)
        gate = jnp.where(jnp.arange(nblk) < own, gate, NEG_INF)
        _, idx = lax.top_k(gate, topk)
        sel_ok = idx < own
        k_sel = kb[bi, hi, idx]
        v_sel = vb[bi, hi, idx]
        s_sel = jnp.einsum('bhqd,bhqrsd->bhqrs', qc, k_sel).astype(jnp.float32) * scale
        kpos_sel = idx[..., None] * MOBA_BLOCK + offs
        s_sel = s_sel - m[..., None] * (t[:, None, None] - kpos_sel).astype(jnp.float32)
        s_sel = jnp.where(sel_ok[..., None], s_sel, NEG_INF).reshape(B, H, MOBA_QCHUNK, topk * MOBA_BLOCK)
        k_own = lax.dynamic_index_in_dim(kb, own, axis=2, keepdims=False)
        v_own = lax.dynamic_index_in_dim(vb, own, axis=2, keepdims=False)
        rel_own = t[:, None] - (own * MOBA_BLOCK + offs)[None, :]
        s_own = jnp.einsum('bhqd,bhsd->bhqs', qc, k_own).astype(jnp.float32) * scale - m * rel_own.astype(jnp.float32)
        s_own = jnp.where(rel_own >= 0, s_own, NEG_INF)
        probs = jax.nn.softmax(jnp.concatenate([s_sel, s_own], axis=-1), axis=-1).astype(v.dtype)
        p_sel = probs[..., :topk * MOBA_BLOCK].reshape(B, H, MOBA_QCHUNK, topk, MOBA_BLOCK)
        p_own = probs[..., topk * MOBA_BLOCK:]
        return (jnp.einsum('bhqrs,bhqrsd->bqhd', p_sel, v_sel)
                + jnp.einsum('bhqs,bhsd->bqhd', p_own, v_own))

    out = lax.map(chunk_attend, (q_chunks, jnp.arange(nq)))
    out = out.transpose(1, 0, 2, 3, 4).reshape(B, S_pad, H, d)
    return out[:, :S]


def gated_delta_rule(q, k, v, g, beta):
    B, S, H, dk = q.shape
    dv = v.shape[-1]
    C = DN_CHUNK
    nc = S // C

    def chunks(t):
        t = t.reshape((B, nc, C, H) + t.shape[3:])
        return jnp.moveaxis(t, 3, 1)

    q = chunks(q) * (dk ** -0.5)
    k = chunks(k)
    v = chunks(v)
    g = jnp.cumsum(chunks(g), axis=-1)
    beta = chunks(beta)
    causal = jnp.tril(jnp.ones((C, C), dtype=bool))
    strict = jnp.tril(jnp.ones((C, C), dtype=bool), -1)
    gdiff = g[..., :, None] - g[..., None, :]
    decay = jnp.where(causal, jnp.exp(jnp.where(causal, gdiff, 0.0)), 0.0)
    k_beta = k * beta[..., None]
    L = jnp.where(strict, jnp.einsum('bhnid,bhnjd->bhnij', k_beta, k) * decay, 0.0)
    rhs = jnp.concatenate([v * beta[..., None], k_beta * jnp.exp(g)[..., None]], axis=-1)
    sol = lax.linalg.triangular_solve(L + jnp.eye(C, dtype=L.dtype), rhs, left_side=True,
                                      lower=True, unit_diagonal=True)
    u, w = sol[..., :dv], sol[..., dv:]
    a_intra = jnp.einsum('bhnid,bhnjd->bhnij', q, k) * decay
    q_dec = q * jnp.exp(g)[..., None]
    g_last = g[..., -1]
    k_dec = k * jnp.exp(g_last[..., None] - g)[..., None]

    def step(state, xs):
        w_c, u_c, qd_c, a_c, kd_c, gl_c = xs
        v_new = u_c - jnp.einsum('bhcd,bhde->bhce', w_c, state)
        o = jnp.einsum('bhcd,bhde->bhce', qd_c, state) + jnp.einsum('bhij,bhje->bhie', a_c, v_new)
        state = state * jnp.exp(gl_c)[..., None, None] + jnp.einsum('bhcd,bhce->bhde', kd_c, v_new)
        return state, o

    xs = (jnp.moveaxis(w, 2, 0), jnp.moveaxis(u, 2, 0), jnp.moveaxis(q_dec, 2, 0),
          jnp.moveaxis(a_intra, 2, 0), jnp.moveaxis(k_dec, 2, 0), jnp.moveaxis(g_last, 2, 0))
    state0 = jnp.zeros((B, H, dk, dv), jnp.float32)
    _, o = lax.scan(step, state0, xs)
    return o.transpose(1, 0, 3, 2, 4).reshape(B, S, H, dv)


def hybrid_token_mixer(h, w_in, swa_sinks, dn_conv_w, dn_a_log, dn_dt_bias, dn_norm, w_out,
                       slopes_swa, slopes_moba):
    B, S, _ = h.shape
    proj = h @ w_in
    split_at = np.cumsum(IN_SPLIT_SIZES)[:-1].tolist()
    qa, ka, va, qb, kb, vb, qkv_c, z_c, b_c, a_c = jnp.split(proj, split_at, axis=-1)
    o_a = swa_sink_attention(qa.reshape(B, S, SWA_HEADS, HEAD_DIM),
                             ka.reshape(B, S, SWA_KV_HEADS, HEAD_DIM),
                             va.reshape(B, S, SWA_KV_HEADS, HEAD_DIM), swa_sinks, slopes_swa)
    o_b = moba_attention(qb.reshape(B, S, MOBA_HEADS, HEAD_DIM), kb.reshape(B, S, MOBA_HEADS, HEAD_DIM),
                         vb.reshape(B, S, MOBA_HEADS, HEAD_DIM), slopes_moba)
    qkv_c = jax.nn.silu(causal_depthwise_conv(qkv_c, dn_conv_w)).astype(jnp.float32)
    qc, kc, vc = jnp.split(qkv_c, 3, axis=-1)
    qc = l2_norm(qc.reshape(B, S, DN_HEADS, DN_HEAD_DIM))
    kc = l2_norm(kc.reshape(B, S, DN_HEADS, DN_HEAD_DIM))
    vc = vc.reshape(B, S, DN_HEADS, DN_HEAD_DIM)
    beta = jax.nn.sigmoid(b_c.astype(jnp.float32))
    g = -jnp.exp(dn_a_log.astype(jnp.float32)) * jax.nn.softplus(a_c.astype(jnp.float32) + dn_dt_bias.astype(jnp.float32))
    o_c = gated_delta_rule(qc, kc, vc, g, beta)
    o_c = rms_norm(o_c, dn_norm) * jax.nn.silu(z_c.astype(jnp.float32).reshape(B, S, DN_HEADS, DN_HEAD_DIM))
    mixed = jnp.concatenate([o_a.reshape(B, S, SWA_Q), o_b.reshape(B, S, MOBA_W),
                             o_c.reshape(B, S, DN_W).astype(h.dtype)], axis=-1)
    return mixed @ w_out


def conv_glu_ffn(h, w_up, conv_w, conv_b, w_down):
    u = causal_depthwise_conv(h @ w_up, conv_w) + conv_b
    gate, up = jnp.split(u, 2, axis=-1)
    return (jax.nn.gelu(gate, approximate=True) * up) @ w_down


def setup_inputs(seed: int = 0) -> dict:
    key = jax.random.key(seed)
    ks = jax.random.split(key, 16)
    f32 = jnp.float32

    def nrm(k, shape, scale):
        return jax.random.normal(k, shape, f32) * scale

    dt = jnp.exp(jax.random.uniform(ks[6], (DEPTH, DN_HEADS), f32)
                 * (math.log(0.1) - math.log(0.001)) + math.log(0.001))
    return {
        'x': nrm(ks[0], (BATCH, SEQ, D_MODEL), 1.0),
        'mix_pre_norm': 1.0 + nrm(ks[1], (DEPTH, D_MODEL), 0.05),
        'w_in': nrm(ks[2], (DEPTH, D_MODEL, IN_WIDTH), D_MODEL ** -0.5),
        'swa_sinks': nrm(ks[3], (DEPTH, SWA_HEADS), 0.5),
        'dn_conv_w': nrm(ks[4], (DEPTH, DN_CONV, 3 * DN_W), DN_CONV ** -0.5),
        'dn_a_log': jnp.log(jax.random.uniform(ks[5], (DEPTH, DN_HEADS), f32, 1.0, 16.0)),
        'dn_dt_bias': dt + jnp.log(-jnp.expm1(-dt)),
        'dn_norm': 1.0 + nrm(ks[7], (DEPTH, DN_HEAD_DIM), 0.05),
        'w_out': nrm(ks[8], (DEPTH, MIX_WIDTH, D_MODEL), MIX_WIDTH ** -0.5),
        'mix_post_norm': 1.0 + nrm(ks[9], (DEPTH, D_MODEL), 0.05),
        'ffn_pre_norm': 1.0 + nrm(ks[10], (DEPTH, D_MODEL), 0.05),
        'w_up': nrm(ks[11], (DEPTH, D_MODEL, 2 * D_FF), D_MODEL ** -0.5),
        'ffn_conv_w': nrm(ks[12], (DEPTH, FFN_CONV, 2 * D_FF), FFN_CONV ** -0.5),
        'ffn_conv_b': nrm(ks[13], (DEPTH, 2 * D_FF), 0.02),
        'w_down': nrm(ks[14], (DEPTH, D_FF, D_MODEL), D_FF ** -0.5),
        'ffn_post_norm': 1.0 + nrm(ks[15], (DEPTH, D_MODEL), 0.05),
    }


def reference(x, mix_pre_norm, w_in, swa_sinks, dn_conv_w, dn_a_log, dn_dt_bias, dn_norm, w_out,
              mix_post_norm, ffn_pre_norm, w_up, ffn_conv_w, ffn_conv_b, w_down, ffn_post_norm):
    slopes_swa, slopes_moba = alibi_slopes()
    for l in range(DEPTH):
        h = rms_norm(x, mix_pre_norm[l])
        mix = hybrid_token_mixer(h, w_in[l], swa_sinks[l], dn_conv_w[l], dn_a_log[l], dn_dt_bias[l],
                                 dn_norm[l], w_out[l], slopes_swa, slopes_moba)
        x = x + rms_norm(mix, mix_post_norm[l])
        h = rms_norm(x, ffn_pre_norm[l])
        x = x + rms_norm(conv_glu_ffn(h, w_up[l], ffn_conv_w[l], ffn_conv_b[l], w_down[l]), ffn_post_norm[l])
    return x
```

```python
import functools
import math

import jax
import jax.numpy as jnp
from jax import lax
from jax.experimental import pallas as pl
from jax.experimental.pallas import tpu as pltpu

D_MODEL = 1024
HEAD_DIM = 64
SWA_HEADS = 4
SWA_KV_HEADS = 2
SWA_WINDOW = 128
SWA_BLOCK = 128
MOBA_HEADS = 4
MOBA_BLOCK = 256
MOBA_TOPK = 3
DN_HEADS = 4
DN_HEAD_DIM = 128
DN_CONV = 4
DN_CHUNK = 64
D_FF = 2816
FFN_CONV = 3
NORM_EPS = 1e-6
NEG_INF = -1e30

SWA_Q = SWA_HEADS * HEAD_DIM
SWA_KV = SWA_KV_HEADS * HEAD_DIM
MOBA_W = MOBA_HEADS * HEAD_DIM
DN_W = DN_HEADS * DN_HEAD_DIM
A_WIDTH = SWA_Q + 2 * SWA_KV
B_WIDTH = 3 * MOBA_W
C_WIDTH = 3 * DN_W
Z_WIDTH = DN_W
BA_WIDTH = 128
IN_MAIN = A_WIDTH + B_WIDTH + C_WIDTH + Z_WIDTH

LANES = 128
HALO = 8
VMEM_LIMIT = 56 * 1024 * 1024

F32 = jnp.float32
BF16 = jnp.bfloat16


def _dot(a, b):
    return jnp.dot(a.astype(BF16), b.astype(BF16), preferred_element_type=F32)


def _dot_nt(a, b):
    return lax.dot_general(a.astype(BF16), b.astype(BF16), (((1,), (1,)), ((), ())), preferred_element_type=F32)


def _dot_tn(a, b):
    return lax.dot_general(a.astype(BF16), b.astype(BF16), (((0,), (0,)), ((), ())), preferred_element_type=F32)


def _dot_f32(a, b):
    return jnp.dot(a, b, preferred_element_type=F32, precision=lax.Precision.HIGHEST)


def _dot_nt_f32(a, b):
    return lax.dot_general(a, b, (((1,), (1,)), ((), ())), preferred_element_type=F32,
                           precision=lax.Precision.HIGHEST)


def _rms(x, w):
    return x * lax.rsqrt(jnp.mean(x * x, axis=-1, keepdims=True) + NORM_EPS) * w


def _sigmoid(x):
    return 1.0 / (1.0 + jnp.exp(-x))


def _iota(shape, dim):
    return lax.broadcasted_iota(jnp.int32, shape, dim)


def _params(*sem):
    return pltpu.CompilerParams(dimension_semantics=sem, vmem_limit_bytes=VMEM_LIMIT)


def _inproj_kernel(x_ref, nw_ref, w_ref, wba_ref, pa_ref, pb_ref, pc_ref, pz_ref, pba_ref):
    h = _rms(x_ref[...], nw_ref[...]).astype(BF16)
    col = 0
    for out_ref, width in ((pa_ref, A_WIDTH), (pb_ref, B_WIDTH), (pc_ref, C_WIDTH), (pz_ref, Z_WIDTH)):
        out_ref[...] = jnp.dot(h, w_ref[:, col:col + width], preferred_element_type=F32).astype(out_ref.dtype)
        col += width
    pba_ref[...] = jnp.dot(h, wba_ref[...], preferred_element_type=F32)


def _inproj(x2, norm_w, w_main, w_ba, tm=512):
    t = x2.shape[0]
    row = lambda w: pl.BlockSpec((tm, w), lambda i: (i, 0))
    full = lambda a: pl.BlockSpec(a.shape, lambda i: (0,) * a.ndim)
    return pl.pallas_call(
        _inproj_kernel,
        grid=(t // tm,),
        in_specs=[row(D_MODEL), full(norm_w), full(w_main), full(w_ba)],
        out_specs=[row(A_WIDTH), row(B_WIDTH), row(C_WIDTH), row(Z_WIDTH), row(BA_WIDTH)],
        out_shape=[jax.ShapeDtypeStruct((t, A_WIDTH), BF16), jax.ShapeDtypeStruct((t, B_WIDTH), BF16),
                   jax.ShapeDtypeStruct((t, C_WIDTH), BF16), jax.ShapeDtypeStruct((t, Z_WIDTH), BF16),
                   jax.ShapeDtypeStruct((t, BA_WIDTH), F32)],
        compiler_params=_params("arbitrary"),
        name="inproj",
    )(x2, norm_w, w_main, w_ba)


def _swa_kernel(sink_ref, cur_ref, prev_ref, o_ref):
    i = pl.program_id(1)
    cur = cur_ref[0]
    prev = prev_ref[0]
    q = cur[:, 0:SWA_Q]
    kk = jnp.concatenate([prev[:, 0:SWA_KV], cur[:, SWA_Q:SWA_Q + SWA_KV]], axis=0)
    vv = jnp.concatenate([prev[:, SWA_KV:2 * SWA_KV], cur[:, SWA_Q + SWA_KV:A_WIDTH]], axis=0)
    shape = (SWA_BLOCK, 2 * SWA_BLOCK)
    rel = _iota(shape, 0) + SWA_BLOCK - _iota(shape, 1)
    valid = (rel >= 0) & (rel < SWA_WINDOW) & ((_iota(shape, 1) >= SWA_BLOCK) | (i > 0))
    relf = rel.astype(F32)
    group = SWA_HEADS // SWA_KV_HEADS
    n_all = SWA_HEADS + MOBA_HEADS
    outs = []
    for h in range(SWA_HEADS):
        kv = h // group
        slope = 2.0 ** (-8.0 * (2 * h + 1) / n_all)
        qh = q[:, h * HEAD_DIM:(h + 1) * HEAD_DIM]
        kh = kk[:, kv * HEAD_DIM:(kv + 1) * HEAD_DIM]
        vh = vv[:, kv * HEAD_DIM:(kv + 1) * HEAD_DIM]
        s = _dot_nt(qh, kh) * (HEAD_DIM ** -0.5) - slope * relf
        s = jnp.where(valid, s, NEG_INF)
        sink = sink_ref[h]
        mx = jnp.maximum(jnp.max(s, axis=-1, keepdims=True), sink)
        p = jnp.exp(s - mx)
        denom = jnp.sum(p, axis=-1, keepdims=True) + jnp.exp(sink - mx)
        outs.append(_dot(p / denom, vh))
    o_ref[0] = jnp.concatenate(outs, axis=-1).astype(o_ref.dtype)


def _swa(pa3, sinks):
    b, s, _ = pa3.shape
    return pl.pallas_call(
        _swa_kernel,
        grid=(b, s // SWA_BLOCK),
        in_specs=[pl.BlockSpec(memory_space=pltpu.SMEM),
                  pl.BlockSpec((1, SWA_BLOCK, A_WIDTH), lambda bi, i: (bi, i, 0)),
                  pl.BlockSpec((1, SWA_BLOCK, 2 * SWA_KV), lambda bi, i: (bi, jnp.maximum(i - 1, 0), 1))],
        out_specs=pl.BlockSpec((1, SWA_BLOCK, SWA_Q), lambda bi, i: (bi, i, 0)),
        out_shape=jax.ShapeDtypeStruct((b, s, SWA_Q), BF16),
        compiler_params=_params("arbitrary", "arbitrary"),
        name="swa",
    )(sinks, pa3, pa3)


XCOL = HEAD_DIM
ACOL = XCOL + 32
BCOL = ACOL + 1


def _moba_slope(h):
    return 2.0 ** (-8.0 * (2 * h + 2) / (SWA_HEADS + MOBA_HEADS))


def _moba_kernel(q_ref, k_ref, v_ref, o_ref, kx_ref, vx_ref, kmean_ref, m_ref, acc_ref):
    i = pl.program_id(1)
    nblk = k_ref.shape[1] // MOBA_BLOCK
    blk = MOBA_BLOCK
    lane = _iota((blk, LANES), 1)
    row = _iota((blk, LANES), 0)
    src = _iota((MOBA_W, LANES), 0)
    dst = _iota((MOBA_W, LANES), 1)
    sel = [((src == dst + h * HEAD_DIM) & (dst < HEAD_DIM)).astype(BF16) for h in range(MOBA_HEADS)]

    @pl.when(i == 0)
    def _build():
        kmean_ref[...] = jnp.zeros_like(kmean_ref)

        def body(n, carry):
            rows = pl.ds(pl.multiple_of(n * blk, blk), blk)
            kb = k_ref[0, rows, :]
            vb = v_ref[0, rows, :]
            for h in range(MOBA_HEADS):
                m = _moba_slope(h)
                kh = jnp.dot(kb, sel[h], preferred_element_type=F32)
                kmean_ref[h, pl.ds(XCOL + n, 1), :] = jnp.mean(kh, axis=0, keepdims=True)
                extra = jnp.where(lane == XCOL + n, 1.0, 0.0)
                extra = jnp.where(lane == ACOL, 1.0, extra)
                extra = jnp.where(lane == BCOL, m * row.astype(F32), extra)
                kx_ref[h, rows, :] = (kh + extra).astype(BF16)
                vh = jnp.dot(vb, sel[h], preferred_element_type=F32)
                vx_ref[h, rows, :] = (vh + jnp.where(lane == HEAD_DIM, 1.0, 0.0)).astype(BF16)
            return carry

        lax.fori_loop(0, nblk, body, 0)

    q = q_ref[0]
    cand = (lane >= XCOL) & (lane < XCOL + i)
    lanef = lane.astype(F32)
    qx = []
    for h in range(MOBA_HEADS):
        m = _moba_slope(h)
        qh = jnp.dot(q, sel[h], preferred_element_type=F32)
        gate = _dot_nt_f32(qh, kmean_ref[h])
        g = jnp.where(cand, gate, -jnp.inf)
        avail = cand
        chosen = jnp.zeros((blk, LANES), jnp.bool_)
        for _ in range(MOBA_TOPK):
            mx = jnp.max(g, axis=-1, keepdims=True)
            first = jnp.min(jnp.where(avail & (g == mx), lanef, 1e9), axis=-1, keepdims=True)
            pick = lanef == first
            chosen = chosen | pick
            avail = avail & jnp.logical_not(pick)
            g = jnp.where(pick, -jnp.inf, g)
        dist = (i + XCOL - lane).astype(F32) * float(blk)
        bias = jnp.where(cand, jnp.where(chosen, 0.0, NEG_INF) - m * dist, 0.0)
        bias = jnp.where(lane == ACOL, -m * row.astype(F32), bias)
        bias = jnp.where(lane == BCOL, 1.0, bias)
        qx.append((qh * (HEAD_DIM ** -0.5) + bias).astype(BF16))

    own = pl.ds(pl.multiple_of(i * blk, blk), blk)
    causal = _iota((blk, blk), 1) <= _iota((blk, blk), 0)
    for h in range(MOBA_HEADS):
        s = _dot_nt(qx[h], kx_ref[h, own, :])
        s = jnp.where(causal, s, NEG_INF)
        mx = jnp.max(s, axis=-1, keepdims=True)
        m_ref[h] = mx
        acc_ref[h] = _dot(jnp.exp(s - mx), vx_ref[h, own, :])

    def past(j, carry):
        rows = pl.ds(pl.multiple_of(j * blk, blk), blk)
        for h in range(MOBA_HEADS):
            s = _dot_nt(qx[h], kx_ref[h, rows, :])
            m_old = m_ref[h]
            m_new = jnp.maximum(m_old, jnp.max(s, axis=-1, keepdims=True))
            m_ref[h] = m_new
            acc_ref[h] = acc_ref[h] * jnp.exp(m_old - m_new) + _dot(jnp.exp(s - m_new), vx_ref[h, rows, :])
        return carry

    lax.fori_loop(0, i, past, 0)

    out = jnp.zeros((blk, MOBA_W), F32)
    for h in range(MOBA_HEADS):
        acc = acc_ref[h]
        o = acc / acc[:, HEAD_DIM:HEAD_DIM + 1]
        osrc = _iota((LANES, MOBA_W), 0)
        odst = _iota((LANES, MOBA_W), 1)
        place = ((odst == osrc + h * HEAD_DIM) & (osrc < HEAD_DIM)).astype(BF16)
        out = out + jnp.dot(o.astype(BF16), place, preferred_element_type=F32)
    o_ref[0] = out.astype(o_ref.dtype)


def _moba(pb3):
    b, s, _ = pb3.shape
    return pl.pallas_call(
        _moba_kernel,
        grid=(b, s // MOBA_BLOCK),
        in_specs=[pl.BlockSpec((1, MOBA_BLOCK, MOBA_W), lambda bi, i: (bi, i, 0)),
                  pl.BlockSpec((1, s, MOBA_W), lambda bi, i: (bi, 0, 1)),
                  pl.BlockSpec((1, s, MOBA_W), lambda bi, i: (bi, 0, 2))],
        out_specs=pl.BlockSpec((1, MOBA_BLOCK, MOBA_W), lambda bi, i: (bi, i, 0)),
        out_shape=jax.ShapeDtypeStruct((b, s, MOBA_W), BF16),
        scratch_shapes=[pltpu.VMEM((MOBA_HEADS, s, LANES), BF16), pltpu.VMEM((MOBA_HEADS, s, LANES), BF16),
                        pltpu.VMEM((MOBA_HEADS, LANES, LANES), F32),
                        pltpu.VMEM((MOBA_HEADS, MOBA_BLOCK, 1), F32),
                        pltpu.VMEM((MOBA_HEADS, MOBA_BLOCK, LANES), F32)],
        compiler_params=_params("arbitrary", "arbitrary"),
        name="moba",
    )(pb3, pb3, pb3)


GDN_TILE = 256


def _gdn_kernel(pc_ref, pz_ref, pba_ref, cw_ref, arow_ref, dtrow_ref, nw_ref, o_ref, ext_ref, state_ref):
    i = pl.program_id(1)
    c = DN_CHUNK
    d = DN_HEAD_DIM

    @pl.when(i == 0)
    def _reset():
        ext_ref[0:HALO, :] = jnp.zeros((HALO, C_WIDTH), F32)
        state_ref[...] = jnp.zeros_like(state_ref)

    ext_ref[HALO:HALO + GDN_TILE, :] = pc_ref[0].astype(F32)
    y = ext_ref[HALO:HALO + GDN_TILE, :] * cw_ref[DN_CONV - 1:DN_CONV, :]
    for t in range(1, DN_CONV):
        y = y + ext_ref[HALO - t:HALO - t + GDN_TILE, :] * cw_ref[DN_CONV - 1 - t:DN_CONV - t, :]
    ext_ref[0:HALO, :] = ext_ref[GDN_TILE:GDN_TILE + HALO, :]
    qkv = y * _sigmoid(y)

    ba = pba_ref[0]
    beta_all = _sigmoid(ba)
    xs = ba + dtrow_ref[...]
    softplus = jnp.maximum(xs, 0.0) + jnp.log(1.0 + jnp.exp(-jnp.abs(xs)))
    g_all = -jnp.exp(arow_ref[...]) * softplus
    z = pz_ref[0].astype(F32)

    ri = _iota((c, c), 0)
    ci = _iota((c, c), 1)
    tril = (ri >= ci).astype(F32)
    strict_up = (ri > ci).astype(F32)
    lower = ri >= ci
    strict = ri > ci

    for ch in range(GDN_TILE // c):
        r0 = ch * c
        gchunk = g_all[r0:r0 + c, :]
        gcum = _dot_f32(tril, gchunk)
        for h in range(DN_HEADS):
            qh = qkv[r0:r0 + c, h * d:(h + 1) * d]
            kh = qkv[r0:r0 + c, DN_W + h * d:DN_W + (h + 1) * d]
            vh = qkv[r0:r0 + c, 2 * DN_W + h * d:2 * DN_W + (h + 1) * d]
            qh = qh * lax.rsqrt(jnp.sum(qh * qh, axis=-1, keepdims=True) + NORM_EPS) * (d ** -0.5)
            kh = kh * lax.rsqrt(jnp.sum(kh * kh, axis=-1, keepdims=True) + NORM_EPS)
            beta = beta_all[r0:r0 + c, h:h + 1]
            gcol = gchunk[:, DN_HEADS + h:DN_HEADS + h + 1]
            gc = gcum[:, DN_HEADS + h:DN_HEADS + h + 1]
            glast = gc[c - 1:c, :]
            diff = _dot_f32(tril, gcol * strict_up)
            decay = jnp.where(lower, jnp.exp(jnp.where(lower, diff, 0.0)), 0.0)
            kbeta = kh * beta
            lmat = jnp.where(strict, _dot_nt(kbeta, kh) * decay, 0.0)
            sol = jnp.concatenate([vh * beta, kbeta * jnp.exp(gc)], axis=-1)
            p = lmat
            sol = sol - _dot(p, sol)
            for _ in range(5):
                p = _dot(p, p)
                sol = sol + _dot(p, sol)
            u = sol[:, 0:d]
            w = sol[:, d:2 * d]
            a_intra = _dot_nt(qh, kh) * decay
            q_dec = qh * jnp.exp(gc)
            k_dec = kh * jnp.exp(glast - gc)
            state = state_ref[h]
            v_new = u - _dot(w, state)
            o = _dot(q_dec, state) + _dot(a_intra, v_new)
            state_ref[h] = state * jnp.exp(glast) + _dot_tn(k_dec, v_new)
            zh = z[r0:r0 + c, h * d:(h + 1) * d]
            o = _rms(o, nw_ref[...]) * (zh * _sigmoid(zh))
            o_ref[0, r0:r0 + c, h * d:(h + 1) * d] = o.astype(o_ref.dtype)


def _gdn(pc3, pz3, pba3, conv_w, a_row, dt_row, norm_w):
    b, s, _ = pc3.shape
    tile = lambda w: pl.BlockSpec((1, GDN_TILE, w), lambda bi, i: (bi, i, 0))
    full = lambda a: pl.BlockSpec(a.shape, lambda bi, i: (0,) * a.ndim)
    return pl.pallas_call(
        _gdn_kernel,
        grid=(b, s // GDN_TILE),
        in_specs=[tile(C_WIDTH), tile(Z_WIDTH), tile(BA_WIDTH), full(conv_w), full(a_row), full(dt_row),
                  full(norm_w)],
        out_specs=tile(DN_W),
        out_shape=jax.ShapeDtypeStruct((b, s, DN_W), BF16),
        scratch_shapes=[pltpu.VMEM((GDN_TILE + HALO, C_WIDTH), F32),
                        pltpu.VMEM((DN_HEADS, DN_HEAD_DIM, DN_HEAD_DIM), F32)],
        compiler_params=_params("arbitrary", "arbitrary"),
        name="gdn",
    )(pc3, pz3, pba3, conv_w, a_row, dt_row, norm_w)


def _outproj_kernel(x_ref, oa_ref, ob_ref, oc_ref, w_ref, postw_ref, prew_ref, xo_ref, h_ref):
    mix = jnp.dot(oa_ref[...], w_ref[0:SWA_Q, :], preferred_element_type=F32)
    mix = mix + jnp.dot(ob_ref[...], w_ref[SWA_Q:SWA_Q + MOBA_W, :], preferred_element_type=F32)
    mix = mix + jnp.dot(oc_ref[...], w_ref[SWA_Q + MOBA_W:, :], preferred_element_type=F32)
    xn = x_ref[...] + _rms(mix, postw_ref[...])
    xo_ref[...] = xn
    h_ref[...] = _rms(xn, prew_ref[...]).astype(h_ref.dtype)


def _outproj(x2, oa, ob, oc, w_out, post_w, pre_w, tm=512):
    t = x2.shape[0]
    row = lambda w: pl.BlockSpec((tm, w), lambda i: (i, 0))
    full = lambda a: pl.BlockSpec(a.shape, lambda i: (0,) * a.ndim)
    return pl.pallas_call(
        _outproj_kernel,
        grid=(t // tm,),
        in_specs=[row(D_MODEL), row(SWA_Q), row(MOBA_W), row(DN_W), full(w_out), full(post_w), full(pre_w)],
        out_specs=[row(D_MODEL), row(D_MODEL)],
        out_shape=[jax.ShapeDtypeStruct((t, D_MODEL), F32), jax.ShapeDtypeStruct((t, D_MODEL), BF16)],
        compiler_params=_params("arbitrary"),
        name="outproj",
    )(x2, oa, ob, oc, w_out, post_w, pre_w)


FF_CHUNK = 256


def _gelu_tanh(x):
    return 0.5 * x * (1.0 + jnp.tanh(math.sqrt(2.0 / math.pi) * (x + 0.044715 * (x * x * x))))


def _ffn_kernel(x_ref, h_ref, wup_ref, cw_ref, cb_ref, wdn_ref, postw_ref, xo_ref, carry_ref, ext_ref,
                *, tm, tiles_per_seq):
    i = pl.program_id(0)

    @pl.when(i % tiles_per_seq == 0)
    def _reset():
        carry_ref[...] = jnp.zeros_like(carry_ref)

    h = h_ref[...]
    acc = jnp.zeros((tm, D_MODEL), F32)
    for j in range(D_FF // FF_CHUNK):
        halves = []
        for part in range(2):
            c0 = part * D_FF + j * FF_CHUNK
            cols = slice(c0, c0 + FF_CHUNK)
            ext_ref[part, 0:HALO, :] = carry_ref[:, cols]
            ext_ref[part, HALO:HALO + tm, :] = jnp.dot(h, wup_ref[:, cols], preferred_element_type=F32)
            carry_ref[:, cols] = ext_ref[part, tm:tm + HALO, :]
            y = ext_ref[part, HALO:HALO + tm, :] * cw_ref[FFN_CONV - 1:FFN_CONV, cols]
            for t in range(1, FFN_CONV):
                y = y + ext_ref[part, HALO - t:HALO - t + tm, :] * cw_ref[FFN_CONV - 1 - t:FFN_CONV - t, cols]
            halves.append(y + cb_ref[:, cols])
        act = (_gelu_tanh(halves[0]) * halves[1]).astype(BF16)
        acc = acc + jnp.dot(act, wdn_ref[j * FF_CHUNK:(j + 1) * FF_CHUNK, :], preferred_element_type=F32)
    xo_ref[...] = x_ref[...] + _rms(acc, postw_ref[...])


def _ffn(x2, h2, w_up, conv_w, conv_b, w_down, post_w, seq, tm=512):
    t = x2.shape[0]
    row = lambda w: pl.BlockSpec((tm, w), lambda i: (i, 0))
    full = lambda a: pl.BlockSpec(a.shape, lambda i: (0,) * a.ndim)
    return pl.pallas_call(
        functools.partial(_ffn_kernel, tm=tm, tiles_per_seq=seq // tm),
        grid=(t // tm,),
        in_specs=[row(D_MODEL), row(D_MODEL), full(w_up), full(conv_w), full(conv_b), full(w_down),
                  full(post_w)],
        out_specs=row(D_MODEL),
        out_shape=jax.ShapeDtypeStruct((t, D_MODEL), F32),
        scratch_shapes=[pltpu.VMEM((HALO, 2 * D_FF), F32), pltpu.VMEM((2, tm + HALO, FF_CHUNK), F32)],
        compiler_params=_params("arbitrary"),
        name="ffn",
    )(x2, h2, w_up, conv_w, conv_b, w_down, post_w)


def kernel(x, mix_pre_norm, w_in, swa_sinks, dn_conv_w, dn_a_log, dn_dt_bias, dn_norm, w_out, mix_post_norm,
           ffn_pre_norm, w_up, ffn_conv_w, ffn_conv_b, w_down, ffn_post_norm):
    b, s, _ = x.shape
    depth = w_in.shape[0]
    x2 = x.reshape(b * s, D_MODEL)
    for l in range(depth):
        w_main = w_in[l, :, :IN_MAIN].astype(BF16)
        w_ba = jnp.pad(w_in[l, :, IN_MAIN:], ((0, 0), (0, BA_WIDTH - 2 * DN_HEADS))).astype(BF16)
        pa, pb, pc, pz, pba = _inproj(x2, mix_pre_norm[l][None, :], w_main, w_ba)
        oa = _swa(pa.reshape(b, s, A_WIDTH), swa_sinks[l])
        ob = _moba(pb.reshape(b, s, B_WIDTH))
        a_row = jnp.pad(dn_a_log[l], (DN_HEADS, BA_WIDTH - 2 * DN_HEADS))[None, :]
        dt_row = jnp.pad(dn_dt_bias[l], (DN_HEADS, BA_WIDTH - 2 * DN_HEADS))[None, :]
        oc = _gdn(pc.reshape(b, s, C_WIDTH), pz.reshape(b, s, Z_WIDTH), pba.reshape(b, s, BA_WIDTH),
                  dn_conv_w[l], a_row, dt_row, dn_norm[l][None, :])
        x2, h2 = _outproj(x2, oa.reshape(b * s, SWA_Q), ob.reshape(b * s, MOBA_W), oc.reshape(b * s, DN_W),
                          w_out[l].astype(BF16), mix_post_norm[l][None, :], ffn_pre_norm[l][None, :])
        x2 = _ffn(x2, h2, w_up[l].astype(BF16), ffn_conv_w[l], ffn_conv_b[l][None, :], w_down[l].astype(BF16),
                  ffn_post_norm[l][None, :], s)
    return x2.reshape(b, s, D_MODEL)
```

```python
import functools
import math

import jax
import jax.numpy as jnp
from jax import lax
from jax.experimental import pallas as pl
from jax.experimental.pallas import tpu as pltpu

D_MODEL = 1024
HEAD_DIM = 64
SWA_HEADS = 4
SWA_KV_HEADS = 2
SWA_WINDOW = 128
SWA_BLOCK = 128
MOBA_HEADS = 4
MOBA_BLOCK = 256
MOBA_TOPK = 3
DN_HEADS = 4
DN_HEAD_DIM = 128
DN_CONV = 4
DN_CHUNK = 64
D_FF = 2816
FFN_CONV = 3
NORM_EPS = 1e-6
NEG_INF = -1e30

SWA_Q = SWA_HEADS * HEAD_DIM
SWA_KV = SWA_KV_HEADS * HEAD_DIM
MOBA_W = MOBA_HEADS * HEAD_DIM
DN_W = DN_HEADS * DN_HEAD_DIM
A_WIDTH = SWA_Q + 2 * SWA_KV
B_WIDTH = 3 * MOBA_W
C_WIDTH = 3 * DN_W
Z_WIDTH = DN_W
BA_WIDTH = 128
IN_MAIN = A_WIDTH + B_WIDTH + C_WIDTH + Z_WIDTH

LANES = 128
HALO = 8
VMEM_LIMIT = 56 * 1024 * 1024

F32 = jnp.float32
BF16 = jnp.bfloat16


def _dot(a, b):
    return jnp.dot(a.astype(BF16), b.astype(BF16), preferred_element_type=F32)


def _dot_nt(a, b):
    return lax.dot_general(a.astype(BF16), b.astype(BF16), (((1,), (1,)), ((), ())), preferred_element_type=F32)


def _dot_tn(a, b):
    return lax.dot_general(a.astype(BF16), b.astype(BF16), (((0,), (0,)), ((), ())), preferred_element_type=F32)


def _dot_f32(a, b):
    return jnp.dot(a, b, preferred_element_type=F32, precision=lax.Precision.HIGHEST)


def _dot_nt_f32(a, b):
    return lax.dot_general(a, b, (((1,), (1,)), ((), ())), preferred_element_type=F32,
                           precision=lax.Precision.HIGHEST)


def _rms(x, w):
    return x * lax.rsqrt(jnp.mean(x * x, axis=-1, keepdims=True) + NORM_EPS) * w


def _sigmoid(x):
    return 1.0 / (1.0 + jnp.exp(-x))


def _iota(shape, dim):
    return lax.broadcasted_iota(jnp.int32, shape, dim)


def _params(*sem):
    return pltpu.CompilerParams(dimension_semantics=sem, vmem_limit_bytes=VMEM_LIMIT)


def _inproj_kernel(x_ref, nw_ref, w_ref, wba_ref, pa_ref, pb_ref, pc_ref, pz_ref, pba_ref):
    h = _rms(x_ref[...], nw_ref[...]).astype(BF16)
    col = 0
    for out_ref, width in ((pa_ref, A_WIDTH), (pb_ref, B_WIDTH), (pc_ref, C_WIDTH), (pz_ref, Z_WIDTH)):
        out_ref[...] = jnp.dot(h, w_ref[:, col:col + width], preferred_element_type=F32).astype(out_ref.dtype)
        col += width
    pba_ref[...] = jnp.dot(h, wba_ref[...], preferred_element_type=F32)


def _inproj(x2, norm_w, w_main, w_ba, tm=512):
    t = x2.shape[0]
    row = lambda w: pl.BlockSpec((tm, w), lambda i: (i, 0))
    full = lambda a: pl.BlockSpec(a.shape, lambda i: (0,) * a.ndim)
    return pl.pallas_call(
        _inproj_kernel,
        grid=(t // tm,),
        in_specs=[row(D_MODEL), full(norm_w), full(w_main), full(w_ba)],
        out_specs=[row(A_WIDTH), row(B_WIDTH), row(C_WIDTH), row(Z_WIDTH), row(BA_WIDTH)],
        out_shape=[jax.ShapeDtypeStruct((t, A_WIDTH), BF16), jax.ShapeDtypeStruct((t, B_WIDTH), BF16),
                   jax.ShapeDtypeStruct((t, C_WIDTH), BF16), jax.ShapeDtypeStruct((t, Z_WIDTH), BF16),
                   jax.ShapeDtypeStruct((t, BA_WIDTH), F32)],
        compiler_params=_params("arbitrary"),
        name="inproj",
    )(x2, norm_w, w_main, w_ba)


def _swa_kernel(sink_ref, cur_ref, prev_ref, o_ref):
    i = pl.program_id(1)
    cur = cur_ref[0]
    prev = prev_ref[0]
    q = cur[:, 0:SWA_Q]
    kk = jnp.concatenate([prev[:, 0:SWA_KV], cur[:, SWA_Q:SWA_Q + SWA_KV]], axis=0)
    vv = jnp.concatenate([prev[:, SWA_KV:2 * SWA_KV], cur[:, SWA_Q + SWA_KV:A_WIDTH]], axis=0)
    shape = (SWA_BLOCK, 2 * SWA_BLOCK)
    rel = _iota(shape, 0) + SWA_BLOCK - _iota(shape, 1)
    valid = (rel >= 0) & (rel < SWA_WINDOW) & ((_iota(shape, 1) >= SWA_BLOCK) | (i > 0))
    relf = rel.astype(F32)
    group = SWA_HEADS // SWA_KV_HEADS
    n_all = SWA_HEADS + MOBA_HEADS
    outs = []
    for h in range(SWA_HEADS):
        kv = h // group
        slope = 2.0 ** (-8.0 * (2 * h + 1) / n_all)
        qh = q[:, h * HEAD_DIM:(h + 1) * HEAD_DIM]
        kh = kk[:, kv * HEAD_DIM:(kv + 1) * HEAD_DIM]
        vh = vv[:, kv * HEAD_DIM:(kv + 1) * HEAD_DIM]
        s = _dot_nt(qh, kh) * (HEAD_DIM ** -0.5) - slope * relf
        s = jnp.where(valid, s, NEG_INF)
        sink = sink_ref[h]
        mx = jnp.maximum(jnp.max(s, axis=-1, keepdims=True), sink)
        p = jnp.exp(s - mx)
        denom = jnp.sum(p, axis=-1, keepdims=True) + jnp.exp(sink - mx)
        outs.append(_dot(p / denom, vh))
    o_ref[0] = jnp.concatenate(outs, axis=-1).astype(o_ref.dtype)


def _swa(pa3, sinks):
    b, s, _ = pa3.shape
    return pl.pallas_call(
        _swa_kernel,
        grid=(b, s // SWA_BLOCK),
        in_specs=[pl.BlockSpec(memory_space=pltpu.SMEM),
                  pl.BlockSpec((1, SWA_BLOCK, A_WIDTH), lambda bi, i: (bi, i, 0)),
                  pl.BlockSpec((1, SWA_BLOCK, 2 * SWA_KV), lambda bi, i: (bi, jnp.maximum(i - 1, 0), 1))],
        out_specs=pl.BlockSpec((1, SWA_BLOCK, SWA_Q), lambda bi, i: (bi, i, 0)),
        out_shape=jax.ShapeDtypeStruct((b, s, SWA_Q), BF16),
        compiler_params=_params("arbitrary", "arbitrary"),
        name="swa",
    )(sinks, pa3, pa3)


XCOL = HEAD_DIM
ACOL = XCOL + 32
BCOL = ACOL + 1


def _moba_slope(h):
    return 2.0 ** (-8.0 * (2 * h + 2) / (SWA_HEADS + MOBA_HEADS))


def _moba_kernel(q_ref, k_ref, v_ref, o_ref, kx_ref, vxt_ref, kmean_ref, acc_ref):
    i = pl.program_id(1)
    nblk = k_ref.shape[1] // MOBA_BLOCK
    blk = MOBA_BLOCK
    lane = _iota((blk, LANES), 1)
    row = _iota((blk, LANES), 0)
    role = _iota((LANES, blk), 0)
    pos = _iota((LANES, blk), 1)
    src = _iota((MOBA_W, LANES), 0)
    dst = _iota((MOBA_W, LANES), 1)
    sel = [((src == dst + h * HEAD_DIM) & (dst < HEAD_DIM)).astype(BF16) for h in range(MOBA_HEADS)]
    sel_t = [((pos == role + h * HEAD_DIM) & (role < HEAD_DIM)).astype(BF16) for h in range(MOBA_HEADS)]

    @pl.when(i == 0)
    def _build():
        kmean_ref[...] = jnp.zeros_like(kmean_ref)

        def body(n, carry):
            rows = pl.ds(pl.multiple_of(n * blk, blk), blk)
            kb = k_ref[0, rows, :]
            vb = v_ref[0, rows, :]
            for h in range(MOBA_HEADS):
                m = _moba_slope(h)
                kh = jnp.dot(kb, sel[h], preferred_element_type=F32)
                kmean_ref[h, pl.ds(XCOL + n, 1), :] = jnp.mean(kh, axis=0, keepdims=True)
                extra = jnp.where(lane == XCOL + n, 1.0, 0.0)
                extra = jnp.where(lane == ACOL, 1.0, extra)
                extra = jnp.where(lane == BCOL, m * row.astype(F32), extra)
                kx_ref[h, n] = (kh + extra).astype(BF16)
                vt = _dot_nt(sel_t[h], vb)
                vxt_ref[h, n] = (vt + jnp.where(role == HEAD_DIM, 1.0, 0.0)).astype(BF16)
            return carry

        lax.fori_loop(0, nblk, body, 0)

    q = q_ref[0]
    cand = (role >= XCOL) & (role < XCOL + i)
    rolef = role.astype(F32)
    qxt = []
    for h in range(MOBA_HEADS):
        m = _moba_slope(h)
        qt = _dot_nt(sel_t[h], q)
        gate = _dot_f32(kmean_ref[h], qt)
        g = jnp.where(cand, gate, -jnp.inf)
        avail = cand
        chosen = jnp.zeros((LANES, blk), jnp.bool_)
        for _ in range(MOBA_TOPK):
            mx = jnp.max(g, axis=0, keepdims=True)
            first = jnp.min(jnp.where(avail & (g == mx), rolef, 1e9), axis=0, keepdims=True)
            pick = rolef == first
            chosen = chosen | pick
            avail = avail & jnp.logical_not(pick)
            g = jnp.where(pick, -jnp.inf, g)
        dist = (i + XCOL - role).astype(F32) * float(blk)
        bias = jnp.where(cand, jnp.where(chosen, 0.0, NEG_INF) - m * dist, 0.0)
        bias = jnp.where(role == ACOL, -m * pos.astype(F32), bias)
        bias = jnp.where(role == BCOL, 1.0, bias)
        qxt.append((qt * (HEAD_DIM ** -0.5) + bias).astype(BF16))

    causal = _iota((blk, blk), 0) <= _iota((blk, blk), 1)
    mx0 = []
    for h in range(MOBA_HEADS):
        s = jnp.dot(kx_ref[h, i], qxt[h], preferred_element_type=F32)
        s = jnp.where(causal, s, NEG_INF)
        mx = jnp.max(s, axis=0, keepdims=True)
        mx0.append(mx)
        acc_ref[h] = jnp.dot(vxt_ref[h, i], jnp.exp(s - mx).astype(BF16), preferred_element_type=F32)

    def past(j, ms):
        s = [jnp.dot(kx_ref[h, j], qxt[h], preferred_element_type=F32) for h in range(MOBA_HEADS)]
        new = [jnp.maximum(ms[h], jnp.max(s[h], axis=0, keepdims=True)) for h in range(MOBA_HEADS)]
        p = [jnp.exp(s[h] - new[h]).astype(BF16) for h in range(MOBA_HEADS)]
        pv = [jnp.dot(vxt_ref[h, j], p[h], preferred_element_type=F32) for h in range(MOBA_HEADS)]
        for h in range(MOBA_HEADS):
            acc_ref[h] = acc_ref[h] * jnp.exp(ms[h] - new[h]) + pv[h]
        return tuple(new)

    lax.fori_loop(0, i, past, tuple(mx0))

    out = jnp.zeros((blk, MOBA_W), F32)
    for h in range(MOBA_HEADS):
        acc = acc_ref[h]
        o = (acc / acc[HEAD_DIM:HEAD_DIM + 1, :]).T
        osrc = _iota((LANES, MOBA_W), 0)
        odst = _iota((LANES, MOBA_W), 1)
        place = ((odst == osrc + h * HEAD_DIM) & (osrc < HEAD_DIM)).astype(BF16)
        out = out + jnp.dot(o.astype(BF16), place, preferred_element_type=F32)
    o_ref[0] = out.astype(o_ref.dtype)


def _moba(pb3):
    b, s, _ = pb3.shape
    return pl.pallas_call(
        _moba_kernel,
        grid=(b, s // MOBA_BLOCK),
        in_specs=[pl.BlockSpec((1, MOBA_BLOCK, MOBA_W), lambda bi, i: (bi, i, 0)),
                  pl.BlockSpec((1, s, MOBA_W), lambda bi, i: (bi, 0, 1)),
                  pl.BlockSpec((1, s, MOBA_W), lambda bi, i: (bi, 0, 2))],
        out_specs=pl.BlockSpec((1, MOBA_BLOCK, MOBA_W), lambda bi, i: (bi, i, 0)),
        out_shape=jax.ShapeDtypeStruct((b, s, MOBA_W), BF16),
        scratch_shapes=[pltpu.VMEM((MOBA_HEADS, s // MOBA_BLOCK, MOBA_BLOCK, LANES), BF16),
                        pltpu.VMEM((MOBA_HEADS, s // MOBA_BLOCK, LANES, MOBA_BLOCK), BF16),
                        pltpu.VMEM((MOBA_HEADS, LANES, LANES), F32),
                        pltpu.VMEM((MOBA_HEADS, LANES, MOBA_BLOCK), F32)],
        compiler_params=_params("arbitrary", "arbitrary"),
        name="moba",
    )(pb3, pb3, pb3)


GDN_TILE = 256


def _gdn_kernel(pc_ref, pz_ref, pba_ref, cw_ref, arow_ref, dtrow_ref, nw_ref, o_ref, ext_ref, state_ref):
    i = pl.program_id(1)
    c = DN_CHUNK
    d = DN_HEAD_DIM

    @pl.when(i == 0)
    def _reset():
        ext_ref[0:HALO, :] = jnp.zeros((HALO, C_WIDTH), F32)
        state_ref[...] = jnp.zeros_like(state_ref)

    ext_ref[HALO:HALO + GDN_TILE, :] = pc_ref[0].astype(F32)
    y = ext_ref[HALO:HALO + GDN_TILE, :] * cw_ref[DN_CONV - 1:DN_CONV, :]
    for t in range(1, DN_CONV):
        y = y + ext_ref[HALO - t:HALO - t + GDN_TILE, :] * cw_ref[DN_CONV - 1 - t:DN_CONV - t, :]
    ext_ref[0:HALO, :] = ext_ref[GDN_TILE:GDN_TILE + HALO, :]
    qkv = y * _sigmoid(y)

    ba = pba_ref[0]
    beta_all = _sigmoid(ba)
    xs = ba + dtrow_ref[...]
    softplus = jnp.maximum(xs, 0.0) + jnp.log(1.0 + jnp.exp(-jnp.abs(xs)))
    g_all = -jnp.exp(arow_ref[...]) * softplus
    z = pz_ref[0].astype(F32)

    ri = _iota((c, c), 0)
    ci = _iota((c, c), 1)
    tril = (ri >= ci).astype(F32)
    strict_up = (ri > ci).astype(F32)
    lower = ri >= ci
    strict = ri > ci

    nch = GDN_TILE // c
    pairs = [(ch, h) for ch in range(nch) for h in range(DN_HEADS)]
    gcum = [_dot_f32(tril, g_all[ch * c:(ch + 1) * c, :]) for ch in range(nch)]
    qs, ks, vs, betas, gcs, glasts, decays = {}, {}, {}, {}, {}, {}, {}
    for ch, h in pairs:
        rows = slice(ch * c, (ch + 1) * c)
        qh = qkv[rows, h * d:(h + 1) * d]
        kh = qkv[rows, DN_W + h * d:DN_W + (h + 1) * d]
        vs[ch, h] = qkv[rows, 2 * DN_W + h * d:2 * DN_W + (h + 1) * d]
        qs[ch, h] = qh * lax.rsqrt(jnp.sum(qh * qh, axis=-1, keepdims=True) + NORM_EPS) * (d ** -0.5)
        ks[ch, h] = kh * lax.rsqrt(jnp.sum(kh * kh, axis=-1, keepdims=True) + NORM_EPS)
        betas[ch, h] = beta_all[rows, h:h + 1]
        gcs[ch, h] = gcum[ch][:, DN_HEADS + h:DN_HEADS + h + 1]
        glasts[ch, h] = gcs[ch, h][c - 1:c, :]
        gcol = g_all[rows, DN_HEADS + h:DN_HEADS + h + 1]
        diff = _dot_f32(tril, gcol * strict_up)
        decays[ch, h] = jnp.where(lower, jnp.exp(jnp.where(lower, diff, 0.0)), 0.0)
    kbeta = {p: ks[p] * betas[p] for p in pairs}
    kk = {p: _dot_nt(jnp.concatenate([kbeta[p], qs[p]], axis=0), ks[p]) for p in pairs}
    pw = {p: jnp.where(strict, kk[p][0:c, :] * decays[p], 0.0).astype(BF16) for p in pairs}
    a_intra = {p: (kk[p][c:2 * c, :] * decays[p]).astype(BF16) for p in pairs}
    sol = {p: jnp.concatenate([vs[p] * betas[p], kbeta[p] * jnp.exp(gcs[p])], axis=-1) for p in pairs}
    sol = {p: sol[p] - _dot(pw[p], sol[p]) for p in pairs}
    for _ in range(5):
        pw = {p: jnp.dot(pw[p], pw[p], preferred_element_type=F32).astype(BF16) for p in pairs}
        sol = {p: sol[p] + _dot(pw[p], sol[p]) for p in pairs}
    q_dec = {p: (qs[p] * jnp.exp(gcs[p])).astype(BF16) for p in pairs}
    k_dec = {p: (ks[p] * jnp.exp(glasts[p] - gcs[p])).astype(BF16) for p in pairs}

    state = [state_ref[h] for h in range(DN_HEADS)]
    for ch in range(nch):
        rows = slice(ch * c, (ch + 1) * c)
        sb = [state[h].astype(BF16) for h in range(DN_HEADS)]
        v_new = [(sol[ch, h][:, 0:d] - _dot(sol[ch, h][:, d:2 * d], sb[h])).astype(BF16)
                 for h in range(DN_HEADS)]
        o = [_dot(q_dec[ch, h], sb[h]) + _dot(a_intra[ch, h], v_new[h]) for h in range(DN_HEADS)]
        state = [state[h] * jnp.exp(glasts[ch, h]) + _dot_tn(k_dec[ch, h], v_new[h]) for h in range(DN_HEADS)]
        for h in range(DN_HEADS):
            zh = z[rows, h * d:(h + 1) * d]
            o_ref[0, rows, h * d:(h + 1) * d] = (_rms(o[h], nw_ref[...]) * (zh * _sigmoid(zh))).astype(o_ref.dtype)
    for h in range(DN_HEADS):
        state_ref[h] = state[h]


def _gdn(pc3, pz3, pba3, conv_w, a_row, dt_row, norm_w):
    b, s, _ = pc3.shape
    tile = lambda w: pl.BlockSpec((1, GDN_TILE, w), lambda bi, i: (bi, i, 0))
    full = lambda a: pl.BlockSpec(a.shape, lambda bi, i: (0,) * a.ndim)
    return pl.pallas_call(
        _gdn_kernel,
        grid=(b, s // GDN_TILE),
        in_specs=[tile(C_WIDTH), tile(Z_WIDTH), tile(BA_WIDTH), full(conv_w), full(a_row), full(dt_row),
                  full(norm_w)],
        out_specs=tile(DN_W),
        out_shape=jax.ShapeDtypeStruct((b, s, DN_W), BF16),
        scratch_shapes=[pltpu.VMEM((GDN_TILE + HALO, C_WIDTH), F32),
                        pltpu.VMEM((DN_HEADS, DN_HEAD_DIM, DN_HEAD_DIM), F32)],
        compiler_params=_params("arbitrary", "arbitrary"),
        name="gdn",
    )(pc3, pz3, pba3, conv_w, a_row, dt_row, norm_w)


def _outproj_kernel(x_ref, oa_ref, ob_ref, oc_ref, w_ref, postw_ref, prew_ref, xo_ref, h_ref):
    mix = jnp.dot(oa_ref[...], w_ref[0:SWA_Q, :], preferred_element_type=F32)
    mix = mix + jnp.dot(ob_ref[...], w_ref[SWA_Q:SWA_Q + MOBA_W, :], preferred_element_type=F32)
    mix = mix + jnp.dot(oc_ref[...], w_ref[SWA_Q + MOBA_W:, :], preferred_element_type=F32)
    xn = x_ref[...] + _rms(mix, postw_ref[...])
    xo_ref[...] = xn
    h_ref[...] = _rms(xn, prew_ref[...]).astype(h_ref.dtype)


def _outproj(x2, oa, ob, oc, w_out, post_w, pre_w, tm=512):
    t = x2.shape[0]
    row = lambda w: pl.BlockSpec((tm, w), lambda i: (i, 0))
    full = lambda a: pl.BlockSpec(a.shape, lambda i: (0,) * a.ndim)
    return pl.pallas_call(
        _outproj_kernel,
        grid=(t // tm,),
        in_specs=[row(D_MODEL), row(SWA_Q), row(MOBA_W), row(DN_W), full(w_out), full(post_w), full(pre_w)],
        out_specs=[row(D_MODEL), row(D_MODEL)],
        out_shape=[jax.ShapeDtypeStruct((t, D_MODEL), F32), jax.ShapeDtypeStruct((t, D_MODEL), BF16)],
        compiler_params=_params("arbitrary"),
        name="outproj",
    )(x2, oa, ob, oc, w_out, post_w, pre_w)


FF_CHUNK = 256


def _gelu_tanh(x):
    return 0.5 * x * (1.0 + jnp.tanh(math.sqrt(2.0 / math.pi) * (x + 0.044715 * (x * x * x))))


def _ffn_kernel(x_ref, h_ref, wup_ref, cw_ref, cb_ref, wdn_ref, postw_ref, xo_ref, carry_ref, ext_ref,
                *, tm, tiles_per_seq):
    i = pl.program_id(0)

    @pl.when(i % tiles_per_seq == 0)
    def _reset():
        carry_ref[...] = jnp.zeros_like(carry_ref)

    h = h_ref[...]
    acc = jnp.zeros((tm, D_MODEL), F32)
    for j in range(D_FF // FF_CHUNK):
        halves = []
        for part in range(2):
            c0 = part * D_FF + j * FF_CHUNK
            cols = slice(c0, c0 + FF_CHUNK)
            ext_ref[part, 0:HALO, :] = carry_ref[:, cols]
            ext_ref[part, HALO:HALO + tm, :] = jnp.dot(h, wup_ref[:, cols], preferred_element_type=F32)
            carry_ref[:, cols] = ext_ref[part, tm:tm + HALO, :]
            y = ext_ref[part, HALO:HALO + tm, :] * cw_ref[FFN_CONV - 1:FFN_CONV, cols]
            for t in range(1, FFN_CONV):
                y = y + ext_ref[part, HALO - t:HALO - t + tm, :] * cw_ref[FFN_CONV - 1 - t:FFN_CONV - t, cols]
            halves.append(y + cb_ref[:, cols])
        act = (_gelu_tanh(halves[0]) * halves[1]).astype(BF16)
        acc = acc + jnp.dot(act, wdn_ref[j * FF_CHUNK:(j + 1) * FF_CHUNK, :], preferred_element_type=F32)
    xo_ref[...] = x_ref[...] + _rms(acc, postw_ref[...])


def _ffn(x2, h2, w_up, conv_w, conv_b, w_down, post_w, seq, tm=512):
    t = x2.shape[0]
    row = lambda w: pl.BlockSpec((tm, w), lambda i: (i, 0))
    full = lambda a: pl.BlockSpec(a.shape, lambda i: (0,) * a.ndim)
    return pl.pallas_call(
        functools.partial(_ffn_kernel, tm=tm, tiles_per_seq=seq // tm),
        grid=(t // tm,),
        in_specs=[row(D_MODEL), row(D_MODEL), full(w_up), full(conv_w), full(conv_b), full(w_down),
                  full(post_w)],
        out_specs=row(D_MODEL),
        out_shape=jax.ShapeDtypeStruct((t, D_MODEL), F32),
        scratch_shapes=[pltpu.VMEM((HALO, 2 * D_FF), F32), pltpu.VMEM((2, tm + HALO, FF_CHUNK), F32)],
        compiler_params=_params("arbitrary"),
        name="ffn",
    )(x2, h2, w_up, conv_w, conv_b, w_down, post_w)


def kernel(x, mix_pre_norm, w_in, swa_sinks, dn_conv_w, dn_a_log, dn_dt_bias, dn_norm, w_out, mix_post_norm,
           ffn_pre_norm, w_up, ffn_conv_w, ffn_conv_b, w_down, ffn_post_norm):
    b, s, _ = x.shape
    depth = w_in.shape[0]
    x2 = x.reshape(b * s, D_MODEL)
    for l in range(depth):
        w_main = w_in[l, :, :IN_MAIN].astype(BF16)
        w_ba = jnp.pad(w_in[l, :, IN_MAIN:], ((0, 0), (0, BA_WIDTH - 2 * DN_HEADS))).astype(BF16)
        pa, pb, pc, pz, pba = _inproj(x2, mix_pre_norm[l][None, :], w_main, w_ba)
        oa = _swa(pa.reshape(b, s, A_WIDTH), swa_sinks[l])
        ob = _moba(pb.reshape(b, s, B_WIDTH))
        a_row = jnp.pad(dn_a_log[l], (DN_HEADS, BA_WIDTH - 2 * DN_HEADS))[None, :]
        dt_row = jnp.pad(dn_dt_bias[l], (DN_HEADS, BA_WIDTH - 2 * DN_HEADS))[None, :]
        oc = _gdn(pc.reshape(b, s, C_WIDTH), pz.reshape(b, s, Z_WIDTH), pba.reshape(b, s, BA_WIDTH),
                  dn_conv_w[l], a_row, dt_row, dn_norm[l][None, :])
        x2, h2 = _outproj(x2, oa.reshape(b * s, SWA_Q), ob.reshape(b * s, MOBA_W), oc.reshape(b * s, DN_W),
                          w_out[l].astype(BF16), mix_post_norm[l][None, :], ffn_pre_norm[l][None, :])
        x2 = _ffn(x2, h2, w_up[l].astype(BF16), ffn_conv_w[l], ffn_conv_b[l][None, :], w_down[l].astype(BF16),
                  ffn_post_norm[l][None, :], s)
    return x2.reshape(b, s, D_MODEL)
```

```python
import functools
import math

import jax
import jax.numpy as jnp
from jax import lax
from jax.experimental import pallas as pl
from jax.experimental.pallas import tpu as pltpu

D_MODEL = 1024
HEAD_DIM = 64
SWA_HEADS = 4
SWA_KV_HEADS = 2
SWA_WINDOW = 128
SWA_BLOCK = 128
MOBA_HEADS = 4
MOBA_BLOCK = 256
MOBA_TOPK = 3
DN_HEADS = 4
DN_HEAD_DIM = 128
DN_CONV = 4
DN_CHUNK = 64
D_FF = 2816
FFN_CONV = 3
NORM_EPS = 1e-6
NEG_INF = -1e30

SWA_Q = SWA_HEADS * HEAD_DIM
SWA_KV = SWA_KV_HEADS * HEAD_DIM
MOBA_W = MOBA_HEADS * HEAD_DIM
DN_W = DN_HEADS * DN_HEAD_DIM
A_WIDTH = SWA_Q + 2 * SWA_KV
B_WIDTH = 3 * MOBA_W
C_WIDTH = 3 * DN_W
Z_WIDTH = DN_W
BA_WIDTH = 128
IN_MAIN = A_WIDTH + B_WIDTH + C_WIDTH + Z_WIDTH

LANES = 128
HALO = 8
VMEM_LIMIT = 56 * 1024 * 1024

F32 = jnp.float32
BF16 = jnp.bfloat16


def _dot(a, b):
    return jnp.dot(a.astype(BF16), b.astype(BF16), preferred_element_type=F32)


def _dot_nt(a, b):
    return lax.dot_general(a.astype(BF16), b.astype(BF16), (((1,), (1,)), ((), ())), preferred_element_type=F32)


def _dot_tn(a, b):
    return lax.dot_general(a.astype(BF16), b.astype(BF16), (((0,), (0,)), ((), ())), preferred_element_type=F32)


def _dot_f32(a, b):
    return jnp.dot(a, b, preferred_element_type=F32, precision=lax.Precision.HIGHEST)


def _dot_nt_f32(a, b):
    return lax.dot_general(a, b, (((1,), (1,)), ((), ())), preferred_element_type=F32,
                           precision=lax.Precision.HIGHEST)


def _rms(x, w):
    return x * lax.rsqrt(jnp.mean(x * x, axis=-1, keepdims=True) + NORM_EPS) * w


def _sigmoid(x):
    return 1.0 / (1.0 + jnp.exp(-x))


def _silu(x):
    hx = 0.5 * x
    return hx + hx * jnp.tanh(hx)


def _iota(shape, dim):
    return lax.broadcasted_iota(jnp.int32, shape, dim)


def _shift_rows(u, prev, t):
    rolled = pltpu.roll(u, t, 0)
    head = jnp.where(_iota(prev.shape, 0) < t, pltpu.roll(prev, t, 0), rolled[0:HALO, :])
    return jnp.concatenate([head, rolled[HALO:, :]], axis=0)


def _resident(a):
    return pl.BlockSpec(a.shape, lambda *_: (0,) * a.ndim, pipeline_mode=pl.Buffered(1))


def _params(*sem, flags=None):
    return pltpu.CompilerParams(dimension_semantics=sem, vmem_limit_bytes=VMEM_LIMIT, flags=flags)


def _inproj_kernel(x_ref, nw_ref, w_ref, wba_ref, cw_ref, pa_ref, pb_ref, pc_ref, pz_ref, pba_ref, carry_ref,
                   *, tm, tiles_per_seq):
    i = pl.program_id(0)

    @pl.when(i % tiles_per_seq == 0)
    def _reset():
        carry_ref[...] = jnp.zeros_like(carry_ref)

    h = _rms(x_ref[...], nw_ref[...]).astype(BF16)
    pa_ref[...] = jnp.dot(h, w_ref[:, 0:A_WIDTH], preferred_element_type=F32).astype(pa_ref.dtype)
    pb_ref[...] = jnp.dot(h, w_ref[:, A_WIDTH:A_WIDTH + B_WIDTH], preferred_element_type=F32).astype(pb_ref.dtype)
    c0 = A_WIDTH + B_WIDTH
    for part in range(3):
        cols = slice(part * DN_W, (part + 1) * DN_W)
        u = jnp.dot(h, w_ref[:, c0 + part * DN_W:c0 + (part + 1) * DN_W], preferred_element_type=F32)
        prev = carry_ref[:, cols]
        carry_ref[:, cols] = u[tm - HALO:tm, :]
        y = u * cw_ref[DN_CONV - 1:DN_CONV, cols]
        for t in range(1, DN_CONV):
            y = y + _shift_rows(u, prev, t) * cw_ref[DN_CONV - 1 - t:DN_CONV - t, cols]
        y = _silu(y)
        if part < 2:
            heads = []
            for hd in range(DN_HEADS):
                yh = y[:, hd * DN_HEAD_DIM:(hd + 1) * DN_HEAD_DIM]
                heads.append(yh * lax.rsqrt(jnp.sum(yh * yh, axis=-1, keepdims=True) + NORM_EPS))
            y = jnp.concatenate(heads, axis=-1)
        pc_ref[:, cols] = y.astype(pc_ref.dtype)
    pz_ref[...] = jnp.dot(h, w_ref[:, c0 + C_WIDTH:IN_MAIN], preferred_element_type=F32).astype(pz_ref.dtype)
    pba_ref[...] = jnp.dot(h, wba_ref[...], preferred_element_type=F32)


def _inproj(x2, norm_w, w_main, w_ba, conv_w, seq, tm=512):
    t = x2.shape[0]
    row = lambda w: pl.BlockSpec((tm, w), lambda i: (i, 0))
    full = _resident
    return pl.pallas_call(
        functools.partial(_inproj_kernel, tm=tm, tiles_per_seq=seq // tm),
        grid=(t // tm,),
        in_specs=[row(D_MODEL), full(norm_w), full(w_main), full(w_ba), full(conv_w)],
        out_specs=[row(A_WIDTH), row(B_WIDTH), row(C_WIDTH), row(Z_WIDTH), row(BA_WIDTH)],
        out_shape=[jax.ShapeDtypeStruct((t, A_WIDTH), BF16), jax.ShapeDtypeStruct((t, B_WIDTH), BF16),
                   jax.ShapeDtypeStruct((t, C_WIDTH), BF16), jax.ShapeDtypeStruct((t, Z_WIDTH), BF16),
                   jax.ShapeDtypeStruct((t, BA_WIDTH), F32)],
        scratch_shapes=[pltpu.VMEM((HALO, C_WIDTH), F32)],
        compiler_params=_params("arbitrary"),
        name="inproj",
    )(x2, norm_w, w_main, w_ba, conv_w)


SWA_STEP_BLOCKS = 4


def _swa_kernel(sink_ref, cur_ref, prev_ref, o_ref):
    i = pl.program_id(1)
    cur = cur_ref[0]
    prev = prev_ref[0]
    q = cur[:, 0:SWA_Q]
    kk = jnp.concatenate([prev[:, 0:SWA_KV], cur[:, SWA_Q:SWA_Q + SWA_KV]], axis=0)
    vv = jnp.concatenate([prev[:, SWA_KV:2 * SWA_KV], cur[:, SWA_Q + SWA_KV:A_WIDTH]], axis=0)
    shape = (SWA_BLOCK, 2 * SWA_BLOCK)
    rel = _iota(shape, 0) + SWA_BLOCK - _iota(shape, 1)
    band = (rel >= 0) & (rel < SWA_WINDOW)
    first = band & ((_iota(shape, 1) >= SWA_BLOCK) | (i > 0))
    relf = rel.astype(F32)
    group = SWA_HEADS // SWA_KV_HEADS
    n_all = SWA_HEADS + MOBA_HEADS
    pairs = [(n, h) for n in range(SWA_STEP_BLOCKS) for h in range(SWA_HEADS)]
    s = {}
    for n, h in pairs:
        kv = h // group
        qh = q[n * SWA_BLOCK:(n + 1) * SWA_BLOCK, h * HEAD_DIM:(h + 1) * HEAD_DIM]
        kh = kk[n * SWA_BLOCK:(n + 2) * SWA_BLOCK, kv * HEAD_DIM:(kv + 1) * HEAD_DIM]
        s[n, h] = _dot_nt(qh, kh)
    probs = {}
    for n, h in pairs:
        slope = 2.0 ** (-8.0 * (2 * h + 1) / n_all)
        sc = s[n, h] * (HEAD_DIM ** -0.5) - slope * relf
        sc = jnp.where(first if n == 0 else band, sc, NEG_INF)
        sink = sink_ref[h]
        mx = jnp.maximum(jnp.max(sc, axis=-1, keepdims=True), sink)
        p = jnp.exp(sc - mx)
        denom = jnp.sum(p, axis=-1, keepdims=True) + jnp.exp(sink - mx)
        probs[n, h] = (p / denom).astype(BF16)
    outs = {}
    for n, h in pairs:
        kv = h // group
        vh = vv[n * SWA_BLOCK:(n + 2) * SWA_BLOCK, kv * HEAD_DIM:(kv + 1) * HEAD_DIM]
        outs[n, h] = jnp.dot(probs[n, h], vh, preferred_element_type=F32)
    for n in range(SWA_STEP_BLOCKS):
        o_ref[0, n * SWA_BLOCK:(n + 1) * SWA_BLOCK, :] = jnp.concatenate(
            [outs[n, h] for h in range(SWA_HEADS)], axis=-1).astype(o_ref.dtype)


def _swa(pa3, sinks):
    b, s, _ = pa3.shape
    rows = SWA_STEP_BLOCKS * SWA_BLOCK
    return pl.pallas_call(
        _swa_kernel,
        grid=(b, s // rows),
        in_specs=[pl.BlockSpec(memory_space=pltpu.SMEM),
                  pl.BlockSpec((1, rows, A_WIDTH), lambda bi, i: (bi, i, 0)),
                  pl.BlockSpec((1, SWA_BLOCK, 2 * SWA_KV),
                               lambda bi, i: (bi, jnp.maximum(i * SWA_STEP_BLOCKS - 1, 0), 1))],
        out_specs=pl.BlockSpec((1, rows, SWA_Q), lambda bi, i: (bi, i, 0)),
        out_shape=jax.ShapeDtypeStruct((b, s, SWA_Q), BF16),
        compiler_params=_params("arbitrary", "arbitrary"),
        name="swa",
    )(sinks, pa3, pa3)


XCOL = HEAD_DIM
ACOL = XCOL + 32
BCOL = ACOL + 1
PAST_GROUP = 4


def _moba_slope(h):
    return 2.0 ** (-8.0 * (2 * h + 2) / (SWA_HEADS + MOBA_HEADS))


def _moba_kernel(q_ref, k_ref, v_ref, o_ref, kx_ref, vxt_ref, kmean_ref, acc_ref):
    i = pl.program_id(1)
    nblk = k_ref.shape[1] // MOBA_BLOCK
    blk = MOBA_BLOCK
    lane = _iota((blk, LANES), 1)
    row = _iota((blk, LANES), 0)
    role = _iota((LANES, blk), 0)
    pos = _iota((LANES, blk), 1)
    src = _iota((MOBA_W, LANES), 0)
    dst = _iota((MOBA_W, LANES), 1)
    sel = [((src == dst + h * HEAD_DIM) & (dst < HEAD_DIM)).astype(BF16) for h in range(MOBA_HEADS)]
    sel_t = [((pos == role + h * HEAD_DIM) & (role < HEAD_DIM)).astype(BF16) for h in range(MOBA_HEADS)]

    @pl.when(i == 0)
    def _build():
        kmean_ref[...] = jnp.zeros_like(kmean_ref)

        def body(n, carry):
            rows = pl.ds(pl.multiple_of(n * blk, blk), blk)
            kb = k_ref[0, rows, :]
            vb = v_ref[0, rows, :]
            for h in range(MOBA_HEADS):
                m = _moba_slope(h)
                kh = jnp.dot(kb, sel[h], preferred_element_type=F32)
                kmean_ref[h, pl.ds(XCOL + n, 1), :] = jnp.mean(kh, axis=0, keepdims=True)
                extra = jnp.where(lane == XCOL + n, 1.0, 0.0)
                extra = jnp.where(lane == ACOL, 1.0, extra)
                extra = jnp.where(lane == BCOL, m * row.astype(F32), extra)
                kx_ref[h, n] = (kh + extra).astype(BF16)
                vt = _dot_nt(sel_t[h], vb)
                vxt_ref[h, n] = (vt + jnp.where(role == HEAD_DIM, 1.0, 0.0)).astype(BF16)
            return carry

        lax.fori_loop(0, nblk, body, 0)

    q = q_ref[0]
    cand = (role >= XCOL) & (role < XCOL + i)
    rolef = role.astype(F32)
    heads = range(MOBA_HEADS)
    qt = [_dot_nt(sel_t[h], q) for h in heads]
    qtb = [qt[h].astype(BF16) for h in heads]
    gate = []
    for h in heads:
        rem = kmean_ref[h]
        g = jnp.zeros((LANES, blk), F32)
        for _ in range(3):
            part = rem.astype(BF16)
            g = g + jnp.dot(part, qtb[h], preferred_element_type=F32)
            rem = rem - part.astype(F32)
        gate.append(g)
    qxt = []
    for h in heads:
        m = _moba_slope(h)
        g = jnp.where(cand, gate[h], -jnp.inf)
        avail = cand
        chosen = jnp.zeros((LANES, blk), jnp.bool_)
        for _ in range(MOBA_TOPK):
            mx = jnp.max(g, axis=0, keepdims=True)
            first = jnp.min(jnp.where(avail & (g == mx), rolef, 1e9), axis=0, keepdims=True)
            pick = rolef == first
            chosen = chosen | pick
            avail = avail & jnp.logical_not(pick)
            g = jnp.where(pick, -jnp.inf, g)
        dist = (i + XCOL - role).astype(F32) * float(blk)
        bias = jnp.where(cand, jnp.where(chosen, 0.0, NEG_INF) - m * dist, 0.0)
        bias = jnp.where(role == ACOL, -m * pos.astype(F32), bias)
        bias = jnp.where(role == BCOL, 1.0, bias)
        qxt.append((qt[h] * (HEAD_DIM ** -0.5) + bias).astype(BF16))

    causal = _iota((blk, blk), 0) <= _iota((blk, blk), 1)
    s0 = [jnp.dot(kx_ref[h, i], qxt[h], preferred_element_type=F32) for h in heads]
    s0 = [jnp.where(causal, s0[h], NEG_INF) for h in heads]
    mx0 = [jnp.max(s0[h], axis=0, keepdims=True) for h in heads]
    p0 = [jnp.exp(s0[h] - mx0[h]).astype(BF16) for h in heads]
    for h in heads:
        acc_ref[h] = jnp.dot(vxt_ref[h, i], p0[h], preferred_element_type=F32)

    def past(j0, nb, ms):
        s = [[jnp.dot(kx_ref[h, j0 + g], qxt[h], preferred_element_type=F32) for g in range(nb)] for h in heads]
        new = []
        for h in heads:
            mx = ms[h]
            for g in range(nb):
                mx = jnp.maximum(mx, jnp.max(s[h][g], axis=0, keepdims=True))
            new.append(mx)
        p = [[jnp.exp(s[h][g] - new[h]).astype(BF16) for g in range(nb)] for h in heads]
        pv = []
        for h in heads:
            t = jnp.dot(vxt_ref[h, j0], p[h][0], preferred_element_type=F32)
            for g in range(1, nb):
                t = t + jnp.dot(vxt_ref[h, j0 + g], p[h][g], preferred_element_type=F32)
            pv.append(t)
        for h in heads:
            acc_ref[h] = acc_ref[h] * jnp.exp(ms[h] - new[h]) + pv[h]
        return tuple(new)

    ms = lax.fori_loop(0, i // PAST_GROUP, lambda jj, ms: past(jj * PAST_GROUP, PAST_GROUP, ms), tuple(mx0))
    lax.fori_loop((i // PAST_GROUP) * PAST_GROUP, i, lambda j, ms: past(j, 1, ms), ms)

    out = jnp.zeros((blk, MOBA_W), F32)
    for h in range(MOBA_HEADS):
        acc = acc_ref[h]
        o = (acc / acc[HEAD_DIM:HEAD_DIM + 1, :]).T
        osrc = _iota((LANES, MOBA_W), 0)
        odst = _iota((LANES, MOBA_W), 1)
        place = ((odst == osrc + h * HEAD_DIM) & (osrc < HEAD_DIM)).astype(BF16)
        out = out + jnp.dot(o.astype(BF16), place, preferred_element_type=F32)
    o_ref[0] = out.astype(o_ref.dtype)


def _moba(pb3):
    b, s, _ = pb3.shape
    return pl.pallas_call(
        _moba_kernel,
        grid=(b, s // MOBA_BLOCK),
        in_specs=[pl.BlockSpec((1, MOBA_BLOCK, MOBA_W), lambda bi, i: (bi, i, 0)),
                  pl.BlockSpec((1, s, MOBA_W), lambda bi, i: (bi, 0, 1)),
                  pl.BlockSpec((1, s, MOBA_W), lambda bi, i: (bi, 0, 2))],
        out_specs=pl.BlockSpec((1, MOBA_BLOCK, MOBA_W), lambda bi, i: (bi, i, 0)),
        out_shape=jax.ShapeDtypeStruct((b, s, MOBA_W), BF16),
        scratch_shapes=[pltpu.VMEM((MOBA_HEADS, s // MOBA_BLOCK, MOBA_BLOCK, LANES), BF16),
                        pltpu.VMEM((MOBA_HEADS, s // MOBA_BLOCK, LANES, MOBA_BLOCK), BF16),
                        pltpu.VMEM((MOBA_HEADS, LANES, LANES), F32),
                        pltpu.VMEM((MOBA_HEADS, LANES, MOBA_BLOCK), F32)],
        compiler_params=_params("arbitrary", "arbitrary"),
        name="moba",
    )(pb3, pb3, pb3)


GDN_TILE = 256


def _gdn_kernel(pc_ref, pz_ref, pba_ref, arow_ref, dtrow_ref, nw_ref, o_ref, state_ref):
    i = pl.program_id(1)
    c = DN_CHUNK
    d = DN_HEAD_DIM

    @pl.when(i == 0)
    def _reset():
        state_ref[...] = jnp.zeros_like(state_ref)

    qkv = pc_ref[0].astype(F32)

    ba = pba_ref[0]
    beta_all = _sigmoid(ba)
    xs = ba + dtrow_ref[...]
    softplus = jnp.maximum(xs, 0.0) + jnp.log(1.0 + jnp.exp(-jnp.abs(xs)))
    g_all = -jnp.exp(arow_ref[...]) * softplus
    z = pz_ref[0].astype(F32)

    ri = _iota((c, c), 0)
    ci = _iota((c, c), 1)
    tril = (ri >= ci).astype(F32)
    strict_up = (ri > ci).astype(F32)
    lower = ri >= ci
    strict = ri > ci

    nch = GDN_TILE // c
    pairs = [(ch, h) for ch in range(nch) for h in range(DN_HEADS)]
    gcum = [_dot_f32(tril, g_all[ch * c:(ch + 1) * c, :]) for ch in range(nch)]
    qs, ks, vs, betas, gcs, glasts, decays = {}, {}, {}, {}, {}, {}, {}
    for ch, h in pairs:
        rows = slice(ch * c, (ch + 1) * c)
        qs[ch, h] = qkv[rows, h * d:(h + 1) * d] * (d ** -0.5)
        ks[ch, h] = qkv[rows, DN_W + h * d:DN_W + (h + 1) * d]
        vs[ch, h] = qkv[rows, 2 * DN_W + h * d:2 * DN_W + (h + 1) * d]
        betas[ch, h] = beta_all[rows, h:h + 1]
        gcs[ch, h] = gcum[ch][:, DN_HEADS + h:DN_HEADS + h + 1]
        glasts[ch, h] = gcs[ch, h][c - 1:c, :]
        gcol = g_all[rows, DN_HEADS + h:DN_HEADS + h + 1]
        diff = _dot_f32(tril, gcol * strict_up)
        decays[ch, h] = jnp.where(lower, jnp.exp(jnp.where(lower, diff, 0.0)), 0.0)
    kbeta = {p: ks[p] * betas[p] for p in pairs}
    kk = {p: _dot_nt(jnp.concatenate([kbeta[p], qs[p]], axis=0), ks[p]) for p in pairs}
    pw = {p: jnp.where(strict, kk[p][0:c, :] * decays[p], 0.0).astype(BF16) for p in pairs}
    a_intra = {p: (kk[p][c:2 * c, :] * decays[p]).astype(BF16) for p in pairs}
    sol = {p: jnp.concatenate([vs[p] * betas[p], kbeta[p] * jnp.exp(gcs[p])], axis=-1) for p in pairs}
    sol = {p: sol[p] - _dot(pw[p], sol[p]) for p in pairs}
    for _ in range(5):
        pw = {p: jnp.dot(pw[p], pw[p], preferred_element_type=F32).astype(BF16) for p in pairs}
        sol = {p: sol[p] + _dot(pw[p], sol[p]) for p in pairs}
    q_dec = {p: (qs[p] * jnp.exp(gcs[p])).astype(BF16) for p in pairs}
    k_dec = {p: (ks[p] * jnp.exp(glasts[p] - gcs[p])).astype(BF16) for p in pairs}

    state = [state_ref[h] for h in range(DN_HEADS)]
    for ch in range(nch):
        rows = slice(ch * c, (ch + 1) * c)
        sb = [state[h].astype(BF16) for h in range(DN_HEADS)]
        v_new = [(sol[ch, h][:, 0:d] - _dot(sol[ch, h][:, d:2 * d], sb[h])).astype(BF16)
                 for h in range(DN_HEADS)]
        o = [_dot(q_dec[ch, h], sb[h]) + _dot(a_intra[ch, h], v_new[h]) for h in range(DN_HEADS)]
        state = [state[h] * jnp.exp(glasts[ch, h]) + _dot_tn(k_dec[ch, h], v_new[h]) for h in range(DN_HEADS)]
        for h in range(DN_HEADS):
            zh = z[rows, h * d:(h + 1) * d]
            o_ref[0, rows, h * d:(h + 1) * d] = (_rms(o[h], nw_ref[...]) * _silu(zh)).astype(o_ref.dtype)
    for h in range(DN_HEADS):
        state_ref[h] = state[h]


def _gdn(pc3, pz3, pba3, a_row, dt_row, norm_w):
    b, s, _ = pc3.shape
    tile = lambda w: pl.BlockSpec((1, GDN_TILE, w), lambda bi, i: (bi, i, 0))
    full = _resident
    return pl.pallas_call(
        _gdn_kernel,
        grid=(b, s // GDN_TILE),
        in_specs=[tile(C_WIDTH), tile(Z_WIDTH), tile(BA_WIDTH), full(a_row), full(dt_row), full(norm_w)],
        out_specs=tile(DN_W),
        out_shape=jax.ShapeDtypeStruct((b, s, DN_W), BF16),
        scratch_shapes=[pltpu.VMEM((DN_HEADS, DN_HEAD_DIM, DN_HEAD_DIM), F32)],
        compiler_params=_params("arbitrary", "arbitrary"),
        name="gdn",
    )(pc3, pz3, pba3, a_row, dt_row, norm_w)


def _outproj_kernel(x_ref, oa_ref, ob_ref, oc_ref, w_ref, postw_ref, prew_ref, xo_ref, h_ref):
    mix = jnp.dot(oa_ref[...], w_ref[0:SWA_Q, :], preferred_element_type=F32)
    mix = mix + jnp.dot(ob_ref[...], w_ref[SWA_Q:SWA_Q + MOBA_W, :], preferred_element_type=F32)
    mix = mix + jnp.dot(oc_ref[...], w_ref[SWA_Q + MOBA_W:, :], preferred_element_type=F32)
    xn = x_ref[...] + _rms(mix, postw_ref[...])
    xo_ref[...] = xn
    h_ref[...] = _rms(xn, prew_ref[...]).astype(h_ref.dtype)


def _outproj(x2, oa, ob, oc, w_out, post_w, pre_w, tm=512):
    t = x2.shape[0]
    row = lambda w: pl.BlockSpec((tm, w), lambda i: (i, 0))
    full = _resident
    return pl.pallas_call(
        _outproj_kernel,
        grid=(t // tm,),
        in_specs=[row(D_MODEL), row(SWA_Q), row(MOBA_W), row(DN_W), full(w_out), full(post_w), full(pre_w)],
        out_specs=[row(D_MODEL), row(D_MODEL)],
        out_shape=[jax.ShapeDtypeStruct((t, D_MODEL), F32), jax.ShapeDtypeStruct((t, D_MODEL), BF16)],
        compiler_params=_params("arbitrary"),
        name="outproj",
    )(x2, oa, ob, oc, w_out, post_w, pre_w)


FF_CHUNK = 256


def _gelu_tanh(x):
    return 0.5 * x * (1.0 + jnp.tanh(math.sqrt(2.0 / math.pi) * (x + 0.044715 * (x * x * x))))


def _ffn_kernel(x_ref, h_ref, wup_ref, cw_ref, cb_ref, wdn_ref, postw_ref, xo_ref, carry_ref,
                *, tm, tiles_per_seq):
    i = pl.program_id(0)

    @pl.when(i % tiles_per_seq == 0)
    def _reset():
        carry_ref[...] = jnp.zeros_like(carry_ref)

    h = h_ref[...]
    n_chunks = D_FF // FF_CHUNK

    def cols_of(j, part):
        c0 = part * D_FF + j * FF_CHUNK
        return slice(c0, c0 + FF_CHUNK)

    def up(j):
        return [jnp.dot(h, wup_ref[:, cols_of(j, part)], preferred_element_type=F32) for part in range(2)]

    acc = jnp.zeros((tm, D_MODEL), F32)
    u_next = up(0)
    for j in range(n_chunks):
        u = u_next
        if j + 1 < n_chunks:
            u_next = up(j + 1)
        halves = []
        for part in range(2):
            cols = cols_of(j, part)
            prev = carry_ref[:, cols]
            carry_ref[:, cols] = u[part][tm - HALO:tm, :]
            y = u[part] * cw_ref[FFN_CONV - 1:FFN_CONV, cols]
            for t in range(1, FFN_CONV):
                y = y + _shift_rows(u[part], prev, t) * cw_ref[FFN_CONV - 1 - t:FFN_CONV - t, cols]
            halves.append(y + cb_ref[:, cols])
        act = (_gelu_tanh(halves[0]) * halves[1]).astype(BF16)
        acc = acc + jnp.dot(act, wdn_ref[j * FF_CHUNK:(j + 1) * FF_CHUNK, :], preferred_element_type=F32)
    xo_ref[...] = x_ref[...] + _rms(acc, postw_ref[...])


def _ffn(x2, h2, w_up, conv_w, conv_b, w_down, post_w, seq, tm=1024):
    t = x2.shape[0]
    row = lambda w: pl.BlockSpec((tm, w), lambda i: (i, 0))
    full = _resident
    return pl.pallas_call(
        functools.partial(_ffn_kernel, tm=tm, tiles_per_seq=seq // tm),
        grid=(t // tm,),
        in_specs=[row(D_MODEL), row(D_MODEL), full(w_up), full(conv_w), full(conv_b), full(w_down),
                  full(post_w)],
        out_specs=row(D_MODEL),
        out_shape=jax.ShapeDtypeStruct((t, D_MODEL), F32),
        scratch_shapes=[pltpu.VMEM((HALO, 2 * D_FF), F32)],
        compiler_params=_params("arbitrary"),
        name="ffn",
    )(x2, h2, w_up, conv_w, conv_b, w_down, post_w)


def kernel(x, mix_pre_norm, w_in, swa_sinks, dn_conv_w, dn_a_log, dn_dt_bias, dn_norm, w_out, mix_post_norm,
           ffn_pre_norm, w_up, ffn_conv_w, ffn_conv_b, w_down, ffn_post_norm):
    b, s, _ = x.shape
    depth = w_in.shape[0]
    x2 = x.reshape(b * s, D_MODEL)
    for l in range(depth):
        w_main = w_in[l, :, :IN_MAIN].astype(BF16)
        w_ba = jnp.pad(w_in[l, :, IN_MAIN:], ((0, 0), (0, BA_WIDTH - 2 * DN_HEADS))).astype(BF16)
        pa, pb, pc, pz, pba = _inproj(x2, mix_pre_norm[l][None, :], w_main, w_ba, dn_conv_w[l], s)
        oa = _swa(pa.reshape(b, s, A_WIDTH), swa_sinks[l])
        ob = _moba(pb.reshape(b, s, B_WIDTH))
        a_row = jnp.pad(dn_a_log[l], (DN_HEADS, BA_WIDTH - 2 * DN_HEADS))[None, :]
        dt_row = jnp.pad(dn_dt_bias[l], (DN_HEADS, BA_WIDTH - 2 * DN_HEADS))[None, :]
        oc = _gdn(pc.reshape(b, s, C_WIDTH), pz.reshape(b, s, Z_WIDTH), pba.reshape(b, s, BA_WIDTH),
                  a_row, dt_row, dn_norm[l][None, :])
        x2, h2 = _outproj(x2, oa.reshape(b * s, SWA_Q), ob.reshape(b * s, MOBA_W), oc.reshape(b * s, DN_W),
                          w_out[l].astype(BF16), mix_post_norm[l][None, :], ffn_pre_norm[l][None, :])
        x2 = _ffn(x2, h2, w_up[l].astype(BF16), ffn_conv_w[l], ffn_conv_b[l][None, :], w_down[l].astype(BF16),
                  ffn_post_norm[l][None, :], s)
    return x2.reshape(b, s, D_MODEL)
```
